```python
import math
import jax, jax.numpy as jnp
from jax import lax
import numpy as np

D_MODEL = 1024
BATCH = 4
SEQ = 4096
DEPTH = 1
DEC_BATCH = 128
DEC_SEQ = 4
PAST_LEN = 16384
PAGE_SIZE = 128

HG_HEADS = 4
HG_DK = 128
HG_DV = 128
HG_FWIDTH = HG_HEADS * HG_DK
HG_WIDTH = HG_HEADS * HG_DV
HG_CHUNK = 64
N_Q = 8
N_KV = 2
HEAD_DIM = 64
GROUP = N_Q // N_KV
WINDOW = 128
ROT_DIM = HEAD_DIM // 4
ROPE_THETA = 500000.0
SWA_WIDTH = N_Q * HEAD_DIM
D_FF = 4 * D_MODEL
EPS = 1e-6
NEG_INF = -1e30
IN_SPLITS = (HG_FWIDTH, HG_FWIDTH, HG_WIDTH, HG_WIDTH, SWA_WIDTH, N_KV * HEAD_DIM, N_KV * HEAD_DIM, D_MODEL, D_MODEL)
D_IN = sum(IN_SPLITS)

kernel_name = "hgrn2_swa_sink_gated_hybrid_step"


def _rmsnorm(x, w):
    xf = x.astype(jnp.float32)
    y = xf * lax.rsqrt(jnp.mean(xf * xf, axis=-1, keepdims=True) + EPS)
    return (y * w.astype(jnp.float32)).astype(x.dtype)


def _rotary(x, pos):
    half = ROT_DIM // 2
    inv = jnp.exp(-math.log(ROPE_THETA) * jnp.arange(half, dtype=jnp.float32) * (2.0 / ROT_DIM))
    ang = pos.astype(jnp.float32)[:, None] * inv[None, :]
    cos = jnp.cos(ang)[:, None, :].astype(x.dtype)
    sin = jnp.sin(ang)[:, None, :].astype(x.dtype)
    x1 = x[..., :half]
    x2 = x[..., half:ROT_DIM]
    return jnp.concatenate([x1 * cos - x2 * sin, x2 * cos + x1 * sin, x[..., ROT_DIM:]], axis=-1)


def _hgrn_recurrence(q, logf, k, v, s0, chunk):
    B, H, T, _ = q.shape
    n = T // chunk
    def blocks(t):
        return t.reshape(B, H, n, chunk, t.shape[-1]).transpose(2, 0, 1, 3, 4)
    causal = jnp.tril(jnp.ones((chunk, chunk), dtype=bool))[:, :, None]
    def step(S, inp):
        qc, lc, kc, vc = inp
        b = jnp.cumsum(lc, axis=-2)
        o_inter = jnp.einsum('bhcd,bhde->bhce', qc * jnp.exp(b), S)
        diff = b[..., :, None, :] - b[..., None, :, :]
        decay = jnp.exp(jnp.where(causal, diff, -jnp.inf))
        attn = jnp.einsum('bhtd,bhtsd,bhsd->bhts', qc, decay, kc)
        o_intra = jnp.einsum('bhts,bhse->bhte', attn, vc)
        bl = b[..., -1:, :]
        S_new = jnp.exp(bl[..., 0, :])[..., None] * S + jnp.einsum('bhsd,bhse->bhde', kc * jnp.exp(bl - b), vc)
        return S_new, o_inter + o_intra
    S, o = lax.scan(step, s0.astype(jnp.float32), (blocks(q), blocks(logf), blocks(k), blocks(v)))
    o = o.transpose(1, 2, 0, 3, 4).reshape(B, H, T, v.shape[-1])
    return o, S


def _sink_attention(q, k, v, mask, sinks):
    s = jnp.einsum('...qhgd,...khd->...hgqk', q.astype(jnp.float32), k.astype(jnp.float32)) * (HEAD_DIM ** -0.5)
    s = jnp.where(mask, s, NEG_INF)
    sink = jnp.broadcast_to(sinks.astype(jnp.float32).reshape(N_KV, GROUP, 1, 1), s.shape[:-1] + (1,))
    p = jax.nn.softmax(jnp.concatenate([s, sink], axis=-1), axis=-1)[..., :-1]
    return jnp.einsum('...hgqk,...khd->...qhgd', p.astype(v.dtype), v)


def _swa_prompt(q, k, v, sinks):
    B, T = q.shape[:2]
    nb = T // WINDOW
    qb = q.reshape(B, nb, WINDOW, N_KV, GROUP, HEAD_DIM)
    kb = k.reshape(B, nb, WINDOW, N_KV, HEAD_DIM)
    vb = v.reshape(B, nb, WINDOW, N_KV, HEAD_DIM)
    padw = ((0, 0), (1, 0), (0, 0), (0, 0), (0, 0))
    k2 = jnp.concatenate([jnp.pad(kb, padw)[:, :-1], kb], axis=2)
    v2 = jnp.concatenate([jnp.pad(vb, padw)[:, :-1], vb], axis=2)
    i = jnp.arange(WINDOW)[:, None]
    j = jnp.arange(2 * WINDOW)[None, :]
    rel = i + WINDOW - j
    kpos = (jnp.arange(nb)[:, None, None] - 1) * WINDOW + j[None]
    mask = (rel >= 0)[None] & (rel < WINDOW)[None] & (kpos >= 0)
    o = _sink_attention(qb, k2, v2, mask[:, None, None], sinks)
    return o.reshape(B, T, SWA_WIDTH)


def _swa_sample(q, k, v, ck, cv, sinks):
    Bd, Tn = q.shape[:2]
    wb = ck.shape[1]
    kall = jnp.concatenate([ck, k], axis=1)
    vall = jnp.concatenate([cv, v], axis=1)
    qpos = PAST_LEN + jnp.arange(Tn)
    kpos = PAST_LEN - wb + jnp.arange(wb + Tn)
    rel = qpos[:, None] - kpos[None, :]
    mask = (rel >= 0) & (rel < WINDOW)
    o = _sink_attention(q.reshape(Bd, Tn, N_KV, GROUP, HEAD_DIM), kall, vall, mask, sinks)
    return o.reshape(Bd, Tn, SWA_WIDTH), kall[:, -wb:], vall[:, -wb:]


def _layer(x, pos, ck, cv, s0, w_in, lb, hg_norm_w, sinks, w_up_a, w_up_b, w_o, n1, n2, w_ff1, w_ff2):
    B, T, _ = x.shape
    h = _rmsnorm(x, n1)
    z = h @ w_in
    hq, hf, hi, hg, sq, sk, sv, ga, gb = jnp.split(z, np.cumsum(IN_SPLITS)[:-1], axis=-1)
    f = lb + (1.0 - lb) * jax.nn.sigmoid(hf.astype(jnp.float32))
    def heads(t, d):
        return t.reshape(B, T, HG_HEADS, d).transpose(0, 2, 1, 3)
    o_hg, s_new = _hgrn_recurrence(heads(hq.astype(jnp.float32), HG_DK), heads(jnp.log(f), HG_DK),
                                   heads(1.0 - f, HG_DK), heads(hi.astype(jnp.float32), HG_DV),
                                   s0, math.gcd(T, HG_CHUNK))
    o_hg = _rmsnorm(o_hg.transpose(0, 2, 1, 3), hg_norm_w)
    y_a = (o_hg.reshape(B, T, HG_WIDTH).astype(x.dtype) * jax.nn.silu(hg)) @ w_up_a
    q = _rotary(sq.reshape(B, T, N_Q, HEAD_DIM), pos)
    k = _rotary(sk.reshape(B, T, N_KV, HEAD_DIM), pos)
    v = sv.reshape(B, T, N_KV, HEAD_DIM)
    if ck is None:
        o_b = _swa_prompt(q, k, v, sinks)
        nk, nv = k[:, -WINDOW:], v[:, -WINDOW:]
    else:
        o_b, nk, nv = _swa_sample(q, k, v, ck, cv, sinks)
    y_b = o_b @ w_up_b
    m = jax.nn.sigmoid(ga) * y_a + jax.nn.sigmoid(gb) * y_b
    x = x + m @ w_o
    h2 = _rmsnorm(x, n2)
    x = x + jnp.square(jax.nn.relu(h2 @ w_ff1)) @ w_ff2
    return x, nk, nv, s_new.astype(x.dtype)


def setup_inputs(seed: int = 0) -> dict:
    key = jax.random.key(seed)
    ks = jax.random.split(key, 18)
    f32 = jnp.float32
    wb = min(WINDOW, PAST_LEN)
    nrm = lambda k, s, sc: jax.random.normal(k, s, f32) * sc
    return {
        "x_prompt": nrm(ks[0], (BATCH, SEQ, D_MODEL), 1.0),
        "x_sample": nrm(ks[1], (DEC_BATCH, DEC_SEQ, D_MODEL), 1.0),
        "cache_swa_k": nrm(ks[2], (DEPTH, DEC_BATCH, wb, N_KV, HEAD_DIM), 1.0),
        "cache_swa_v": nrm(ks[3], (DEPTH, DEC_BATCH, wb, N_KV, HEAD_DIM), 1.0),
        "state_hgrn": nrm(ks[4], (DEPTH, DEC_BATCH, HG_HEADS, HG_DK, HG_DV), 0.3),
        "w_in": nrm(ks[5], (DEPTH, D_MODEL, D_IN), D_MODEL ** -0.5),
        "hgrn_lb_logits": nrm(ks[6], (DEPTH + 1, HG_FWIDTH), 0.5),
        "hgrn_norm_w": 1.0 + nrm(ks[7], (DEPTH, HG_HEADS, HG_DV), 0.02),
        "sinks": nrm(ks[8], (DEPTH, N_Q), 1.0),
        "w_up_a": nrm(ks[9], (DEPTH, HG_WIDTH, D_MODEL), HG_WIDTH ** -0.5),
        "w_up_b": nrm(ks[10], (DEPTH, SWA_WIDTH, D_MODEL), SWA_WIDTH ** -0.5),
        "w_o": nrm(ks[11], (DEPTH, D_MODEL, D_MODEL), D_MODEL ** -0.5),
        "norm1_w": 1.0 + nrm(ks[12], (DEPTH, D_MODEL), 0.02),
        "norm2_w": 1.0 + nrm(ks[13], (DEPTH, D_MODEL), 0.02),
        "w_ff1": nrm(ks[14], (DEPTH, D_MODEL, D_FF), D_MODEL ** -0.5),
        "w_ff2": nrm(ks[15], (DEPTH, D_FF, D_MODEL), D_FF ** -0.5),
        "normf_w": 1.0 + nrm(ks[16], (D_MODEL,), 0.02),
    }


def reference(x_prompt, x_sample, cache_swa_k, cache_swa_v, state_hgrn, w_in, hgrn_lb_logits, hgrn_norm_w,
              sinks, w_up_a, w_up_b, w_o, norm1_w, norm2_w, w_ff1, w_ff2, normf_w):
    lb_all = jnp.cumsum(jax.nn.softmax(hgrn_lb_logits.astype(jnp.float32), axis=0), axis=0)
    bp, tp = x_prompt.shape[:2]
    pos_p = jnp.arange(tp)
    pos_s = PAST_LEN + jnp.arange(x_sample.shape[1])
    xp, xs = x_prompt, x_sample
    kp_l, vp_l, sp_l, ks_l, vs_l, ss_l = [], [], [], [], [], []
    for l in range(DEPTH):
        wl = (w_in[l], lb_all[l], hgrn_norm_w[l], sinks[l], w_up_a[l], w_up_b[l], w_o[l],
              norm1_w[l], norm2_w[l], w_ff1[l], w_ff2[l])
        s0 = jnp.zeros((bp, HG_HEADS, HG_DK, HG_DV), jnp.float32)
        xp, kp, vp, sp = _layer(xp, pos_p, None, None, s0, *wl)
        xs, ksn, vsn, ssn = _layer(xs, pos_s, cache_swa_k[l], cache_swa_v[l], state_hgrn[l], *wl)
        kp_l.append(kp); vp_l.append(vp); sp_l.append(sp)
        ks_l.append(ksn); vs_l.append(vsn); ss_l.append(ssn)
    y_prompt = _rmsnorm(xp, normf_w)
    y_sample = _rmsnorm(xs, normf_w)
    return (y_prompt, y_sample, jnp.stack(kp_l), jnp.stack(vp_l), jnp.stack(sp_l),
            jnp.stack(ks_l), jnp.stack(vs_l), jnp.stack(ss_l))
```

```python
import functools
import math

import jax
import jax.numpy as jnp
from jax import lax
from jax.experimental import pallas as pl
from jax.experimental.pallas import tpu as pltpu

D_MODEL = 1024
PAST_LEN = 16384
HG_HEADS = 4
HG_DK = 128
HG_DV = 128
HG_W = HG_HEADS * HG_DK
N_Q = 8
N_KV = 2
HEAD_DIM = 64
WINDOW = 128
ROT_DIM = HEAD_DIM // 4
ROT_HALF = ROT_DIM // 2
ROPE_THETA = 500000.0
SWA_W = N_Q * HEAD_DIM
KV_W = N_KV * HEAD_DIM
D_FF = 4 * D_MODEL
EPS = 1e-6
NEG_INF = -1e30

C_HQ, C_HF, C_HI, C_HG = 0, 512, 1024, 1536
C_SQ, C_SK, C_SV = 2048, 2560, 2688
C_GA, C_GB = 2816, 3840
D_IN = 4864

LANES = 128
CHUNK = 128
TM = 256
TM_FFN = 512
BB = 8
VMEM_LIMIT = 56 * 1024 * 1024

F32 = jnp.float32
BF16 = jnp.bfloat16


def _mm(a, b):
    return jnp.dot(a, b, preferred_element_type=F32)


def _mm_nt(a, b):
    return lax.dot_general(a, b, (((1,), (1,)), ((), ())), preferred_element_type=F32)


def _mm_tn(a, b):
    return lax.dot_general(a, b, (((0,), (0,)), ((), ())), preferred_element_type=F32)


def _sigmoid(x):
    return 1.0 / (1.0 + jnp.exp(-x))


def _rms(x, w):
    return x * lax.rsqrt(jnp.mean(x * x, axis=-1, keepdims=True) + EPS) * w


def _row_to_col(row):
    n = row.shape[1]
    eye = lax.broadcasted_iota(jnp.int32, (n, n), 0) == lax.broadcasted_iota(jnp.int32, (n, n), 1)
    return jnp.sum(jnp.where(eye, jnp.broadcast_to(row, (n, n)), 0.0), axis=1, keepdims=True)


def _rotary(x, cos_t, sin_a, sin_b):
    return x * cos_t + pltpu.roll(x, LANES - ROT_HALF, 1) * sin_a + pltpu.roll(x, ROT_HALF, 1) * sin_b


def _forget(hf, lb):
    f = lb + (1.0 - lb) * _sigmoid(hf)
    return jnp.log(f), 1.0 - f


def _hgrn_chunk(q, hf, v, lb, s_ref, b_ref):
    c = CHUNK
    logf, k = _forget(hf, lb)
    ri = lax.broadcasted_iota(jnp.int32, (c, c), 0)
    ci = lax.broadcasted_iota(jnp.int32, (c, c), 1)
    tri = (ri >= ci).astype(BF16)
    l_hi = logf.astype(BF16)
    l_lo = (logf - l_hi.astype(F32)).astype(BF16)
    b = _mm(jnp.concatenate([tri, tri], axis=1), jnp.concatenate([l_hi, l_lo], axis=0))
    b_ref[...] = b
    b_last = b_ref[pl.ds(c - 1, 1), :]

    qb = (q * jnp.exp(b)).astype(BF16)
    kh = (k * jnp.exp(b_last - b)).astype(BF16)
    vb = v.astype(BF16)
    q16 = q.astype(BF16)
    k16 = k.astype(BF16)

    rowf = lax.broadcasted_iota(jnp.int32, (c, HG_W), 0)
    a = []
    for h in range(HG_HEADS):
        sl = slice(h * HG_DK, (h + 1) * HG_DK)
        a.append(jnp.where(ri == ci, _mm_nt(q16[:, sl], k16[:, sl]), 0.0))
    m = 1
    while m < c:
        blk = 2 * m
        if blk >= 16:
            parts = []
            for i in range(c // blk):
                parts.append(jnp.broadcast_to(b_ref[pl.ds(i * blk + m - 1, 1), :], (blk, HG_W)))
            r = jnp.concatenate(parts, axis=0) if len(parts) > 1 else parts[0]
        else:
            pos = rowf & (blk - 1)
            r = b
            for off in range(-(m - 1), m + 1):
                if off == 0:
                    continue
                r = jnp.where(pos - (m - 1) == off, pltpu.roll(b, off % c, 0), r)
        x = jnp.exp(-jnp.abs(b - r))
        qx = (q * x).astype(BF16)
        kx = (k * x).astype(BF16)
        sh = blk.bit_length() - 1
        mask = ((ri >> sh) == (ci >> sh)) & ((ri & (blk - 1)) >= m) & ((ci & (blk - 1)) < m)
        for h in range(HG_HEADS):
            sl = slice(h * HG_DK, (h + 1) * HG_DK)
            a[h] = jnp.where(mask, _mm_nt(qx[:, sl], kx[:, sl]), a[h])
        m = blk

    dec = jnp.exp(b_last)
    outs = []
    for h in range(HG_HEADS):
        sl = slice(h * HG_DK, (h + 1) * HG_DK)
        s_old = s_ref[h]
        outs.append(_mm(qb[:, sl], s_old.astype(BF16)) + _mm(a[h].astype(BF16), vb[:, sl]))
        s_ref[h] = _row_to_col(dec[:, sl]) * s_old + _mm_tn(kh[:, sl], vb[:, sl])
    return jnp.concatenate(outs, axis=1)


def _hgrn_norm_gate(o, hg, hgw):
    parts = []
    for h in range(HG_HEADS):
        sl = slice(h * HG_DV, (h + 1) * HG_DV)
        parts.append(_rms(o[:, sl], hgw[:, sl]))
    on = jnp.concatenate(parts, axis=1)
    return on * (hg * _sigmoid(hg))


def _split_heads(kv, kv_rolled, g):
    lo = lax.broadcasted_iota(jnp.int32, kv.shape, 1) < HEAD_DIM
    if g == 0:
        return jnp.where(lo, kv, 0.0), jnp.where(lo, 0.0, kv_rolled)
    return jnp.where(lo, kv_rolled, 0.0), jnp.where(lo, 0.0, kv)


def _masked_softmax(s, valid, sink):
    s = jnp.where(valid, s, NEG_INF)
    mx = jnp.maximum(jnp.max(s, axis=1, keepdims=True), sink)
    e = jnp.exp(s - mx)
    den = jnp.sum(e, axis=1, keepdims=True) + jnp.exp(sink - mx)
    return e * (1.0 / den)


def _swa_block(q, kw, vw, sinks_ref, kpos0):
    c = CHUNK
    kr = pltpu.roll(kw, HEAD_DIM, 1)
    vr = pltpu.roll(vw, HEAD_DIM, 1)
    ri = lax.broadcasted_iota(jnp.int32, (2 * c, 2 * c), 0)
    ci = lax.broadcasted_iota(jnp.int32, (2 * c, 2 * c), 1)
    rel = ci - (ri & (c - 1))
    valid = (rel >= 1) & (rel <= c) & (ci + kpos0 >= 0)
    top = lax.broadcasted_iota(jnp.int32, (2 * c, 1), 0) < c
    tiles = []
    for g in range(N_KV):
        ka, kb = _split_heads(kw, kr, g)
        va, vb = _split_heads(vw, vr, g)
        kcat = jnp.concatenate([ka, kb], axis=0).astype(BF16)
        vcat = jnp.concatenate([va, vb], axis=0).astype(BF16)
        qg = jnp.concatenate([q[:, 2 * g * LANES:(2 * g + 1) * LANES],
                              q[:, (2 * g + 1) * LANES:(2 * g + 2) * LANES]], axis=0).astype(BF16)
        s = _mm_nt(qg, kcat)
        sink_a = jnp.where(top, sinks_ref[4 * g], sinks_ref[4 * g + 2])
        sink_b = jnp.where(top, sinks_ref[4 * g + 1], sinks_ref[4 * g + 3])
        pa = _masked_softmax(s[:, :2 * c], valid, sink_a)
        pb = _masked_softmax(s[:, 2 * c:], valid, sink_b)
        og = _mm(jnp.concatenate([pa, pb], axis=1).astype(BF16), vcat)
        tiles.append(og[:c])
        tiles.append(og[c:])
    return jnp.concatenate(tiles, axis=1)


def _merge_out(x, ya_in, o_b, ga, gb, wua_ref, wub_ref, wo_ref):
    ya = _mm(ya_in.astype(BF16), wua_ref[...])
    yb = _mm(o_b.astype(BF16), wub_ref[...])
    m = _sigmoid(ga) * ya + _sigmoid(gb) * yb
    return x + _mm(m.astype(BF16), wo_ref[...])


def _ffn_final(x1, n2, w1_ref, w2_ref, nf):
    h2 = _rms(x1, n2).astype(BF16)
    u = jnp.maximum(_mm(h2, w1_ref[...]), 0.0)
    x2 = x1 + _mm((u * u).astype(BF16), w2_ref[...])
    return _rms(x2, nf)


def _mixer_prompt_kernel(sinks_ref, x_ref, cos_ref, sa_ref, sb_ref, n1_ref, win_ref, lb_ref, hgw_ref,
                         wua_ref, wub_ref, wo_ref,
                         x1_ref, nk_ref, nv_ref, ns_ref,
                         s_ref, kprev_ref, vprev_ref, b_ref):
    t = pl.program_id(1)

    @pl.when(t == 0)
    def _():
        s_ref[...] = jnp.zeros_like(s_ref)
        kprev_ref[...] = jnp.zeros_like(kprev_ref)
        vprev_ref[...] = jnp.zeros_like(vprev_ref)

    x = x_ref[0]
    h = _rms(x, n1_ref[...]).astype(BF16)
    lb = lb_ref[...]

    zh = _mm(h, win_ref[:, C_HQ:C_SQ])
    o_parts = []
    for c in range(TM // CHUNK):
        rs = slice(c * CHUNK, (c + 1) * CHUNK)
        o_parts.append(_hgrn_chunk(zh[rs, C_HQ:C_HF], zh[rs, C_HF:C_HI], zh[rs, C_HI:C_HG],
                                   lb, s_ref, b_ref))
    ya_in = _hgrn_norm_gate(jnp.concatenate(o_parts, axis=0), zh[:, C_HG:C_SQ], hgw_ref[...])

    zs = _mm(h, win_ref[:, C_SQ:C_GA])
    cos_t, sin_a, sin_b = cos_ref[...], sa_ref[...], sb_ref[...]
    q = jnp.concatenate([_rotary(zs[:, i * LANES:(i + 1) * LANES], cos_t, sin_a, sin_b)
                         for i in range(SWA_W // LANES)], axis=1) * (HEAD_DIM ** -0.5)
    k = _rotary(zs[:, SWA_W:SWA_W + KV_W], cos_t, sin_a, sin_b)
    v = zs[:, SWA_W + KV_W:]
    k_all = jnp.concatenate([kprev_ref[...], k], axis=0)
    v_all = jnp.concatenate([vprev_ref[...], v], axis=0)
    ob_parts = []
    for j in range(TM // CHUNK):
        ws = slice(j * CHUNK, (j + 2) * CHUNK)
        ob_parts.append(_swa_block(q[j * CHUNK:(j + 1) * CHUNK], k_all[ws], v_all[ws], sinks_ref,
                                   t * TM + (j - 1) * CHUNK))
    o_b = jnp.concatenate(ob_parts, axis=0)
    k_tail = k[TM - CHUNK:]
    v_tail = v[TM - CHUNK:]
    kprev_ref[...] = k_tail
    vprev_ref[...] = v_tail
    nk_ref[0] = k_tail
    nv_ref[0] = v_tail

    zg = _mm(h, win_ref[:, C_GA:])
    x1_ref[0] = _merge_out(x, ya_in, o_b, zg[:, :D_MODEL], zg[:, D_MODEL:], wua_ref, wub_ref, wo_ref)
    ns_ref[0] = s_ref[...]


def _const_spec(shape, grid_rank):
    zeros = (0,) * len(shape)
    if grid_rank == 1:
        return pl.BlockSpec(shape, lambda i: zeros, pipeline_mode=pl.Buffered(1))
    return pl.BlockSpec(shape, lambda i, j: zeros, pipeline_mode=pl.Buffered(1))


def _smem_spec():
    return pl.BlockSpec(memory_space=pltpu.SMEM)


def _mixer_prompt(x, sinks, tabs, n1, w_in, lb, hgw, wua, wub, wo):
    bsz, seq, _ = x.shape
    cos_t, sin_a, sin_b = tabs
    grid = (bsz, seq // TM)
    tab_spec = pl.BlockSpec((TM, LANES), lambda b, t: (t, 0))
    return pl.pallas_call(
        _mixer_prompt_kernel,
        grid=grid,
        in_specs=[
            _smem_spec(),
            pl.BlockSpec((1, TM, D_MODEL), lambda b, t: (b, t, 0)),
            tab_spec, tab_spec, tab_spec,
            _const_spec((1, D_MODEL), 2),
            _const_spec((D_MODEL, D_IN), 2),
            _const_spec((1, HG_W), 2),
            _const_spec((1, HG_W), 2),
            _const_spec((HG_W, D_MODEL), 2),
            _const_spec((SWA_W, D_MODEL), 2),
            _const_spec((D_MODEL, D_MODEL), 2),
        ],
        out_specs=[
            pl.BlockSpec((1, TM, D_MODEL), lambda b, t: (b, t, 0)),
            pl.BlockSpec((1, WINDOW, KV_W), lambda b, t: (b, 0, 0)),
            pl.BlockSpec((1, WINDOW, KV_W), lambda b, t: (b, 0, 0)),
            pl.BlockSpec((1, HG_HEADS, HG_DK, HG_DV), lambda b, t: (b, 0, 0, 0)),
        ],
        out_shape=[
            jax.ShapeDtypeStruct((bsz, seq, D_MODEL), F32),
            jax.ShapeDtypeStruct((bsz, WINDOW, KV_W), F32),
            jax.ShapeDtypeStruct((bsz, WINDOW, KV_W), F32),
            jax.ShapeDtypeStruct((bsz, HG_HEADS, HG_DK, HG_DV), F32),
        ],
        scratch_shapes=[
            pltpu.VMEM((HG_HEADS, HG_DK, HG_DV), F32),
            pltpu.VMEM((CHUNK, KV_W), F32),
            pltpu.VMEM((CHUNK, KV_W), F32),
            pltpu.VMEM((CHUNK, HG_W), F32),
        ],
        compiler_params=pltpu.CompilerParams(
            dimension_semantics=("arbitrary", "arbitrary"), vmem_limit_bytes=VMEM_LIMIT),
        name="mixer_prompt",
    )(sinks, x, cos_t, sin_a, sin_b, n1, w_in, lb, hgw, wua, wub, wo)


def _ffn_kernel(x_ref, n2_ref, w1_ref, w2_ref, nf_ref, y_ref):
    y_ref[...] = _ffn_final(x_ref[...], n2_ref[...], w1_ref, w2_ref, nf_ref[...])


def _ffn(x1, n2, w1, w2, nf):
    rows = x1.shape[0]
    return pl.pallas_call(
        _ffn_kernel,
        grid=(rows // TM_FFN,),
        in_specs=[
            pl.BlockSpec((TM_FFN, D_MODEL), lambda i: (i, 0)),
            _const_spec((1, D_MODEL), 1),
            _const_spec((D_MODEL, D_FF), 1),
            _const_spec((D_FF, D_MODEL), 1),
            _const_spec((1, D_MODEL), 1),
        ],
        out_specs=pl.BlockSpec((TM_FFN, D_MODEL), lambda i: (i, 0)),
        out_shape=jax.ShapeDtypeStruct((rows, D_MODEL), F32),
        compiler_params=pltpu.CompilerParams(
            dimension_semantics=("arbitrary",), vmem_limit_bytes=VMEM_LIMIT),
        name="ffn",
    )(x1, n2, w1, w2, nf)


def _sample_in_kernel(x_ref, n1_ref, win_ref, z_ref):
    h = _rms(x_ref[...], n1_ref[...]).astype(BF16)
    z_ref[...] = _mm(h, win_ref[...])


def _sample_in(x, n1, w_in):
    rows = x.shape[0]
    return pl.pallas_call(
        _sample_in_kernel,
        grid=(1,),
        in_specs=[
            pl.BlockSpec((rows, D_MODEL), lambda i: (0, 0)),
            _const_spec((1, D_MODEL), 1),
            _const_spec((D_MODEL, D_IN), 1),
        ],
        out_specs=pl.BlockSpec((rows, D_IN), lambda i: (0, 0)),
        out_shape=jax.ShapeDtypeStruct((rows, D_IN), F32),
        compiler_params=pltpu.CompilerParams(
            dimension_semantics=("arbitrary",), vmem_limit_bytes=VMEM_LIMIT),
        name="sample_in",
    )(x, n1, w_in)


def _sample_core_kernel(sinks_ref, z_ref, ck_ref, cv_ref, st_ref, cos_ref, sa_ref, sb_ref, lb_ref,
                        ohg_ref, ob_ref, nk_ref, nv_ref, ns_ref):
    tn = z_ref.shape[1]
    wb = ck_ref.shape[1]
    lb = lb_ref[...]
    cos_t, sin_a, sin_b = cos_ref[...], sa_ref[...], sb_ref[...]
    rows = lax.broadcasted_iota(jnp.int32, (tn, HG_W), 0)
    lane_lo = lax.broadcasted_iota(jnp.int32, (tn, LANES), 1) < HEAD_DIM
    cvalid = (lax.broadcasted_iota(jnp.int32, (tn, wb), 1) >=
              lax.broadcasted_iota(jnp.int32, (tn, wb), 0) + 1 + (wb - WINDOW))
    trow = lax.broadcasted_iota(jnp.int32, (tn, 1), 0)

    def body(b, carry):
        zb = z_ref[b]
        q = zb[:, C_HQ:C_HF]
        logf, k = _forget(zb[:, C_HF:C_HI], lb)
        v = zb[:, C_HI:C_HG]
        bc = jnp.zeros((tn, HG_W), F32)
        for u in range(tn):
            bc = bc + jnp.where(rows >= u, jnp.broadcast_to(logf[u:u + 1, :], (tn, HG_W)), 0.0)
        b_last = bc[tn - 1:tn, :]
        qb = (q * jnp.exp(bc)).astype(BF16)
        kh = (k * jnp.exp(b_last - bc)).astype(BF16)
        vb16 = v.astype(BF16)
        dec = jnp.exp(b_last)
        o_intra = [jnp.zeros((tn, HG_DV), F32) for _ in range(HG_HEADS)]
        for s in range(tn):
            ok = rows >= s
            w = jnp.where(ok, jnp.exp(jnp.where(ok, bc - jnp.broadcast_to(bc[s:s + 1, :], (tn, HG_W)), 0.0)), 0.0)
            prod = q * jnp.broadcast_to(k[s:s + 1, :], (tn, HG_W)) * w
            vs = jnp.broadcast_to(v[s:s + 1, :], (tn, HG_W))
            for h in range(HG_HEADS):
                sl = slice(h * HG_DK, (h + 1) * HG_DK)
                o_intra[h] = o_intra[h] + jnp.sum(prod[:, sl], axis=1, keepdims=True) * vs[:, sl]
        o_parts = []
        for h in range(HG_HEADS):
            sl = slice(h * HG_DK, (h + 1) * HG_DK)
            s_old = st_ref[b, h]
            o_parts.append(_mm(qb[:, sl], s_old.astype(BF16)) + o_intra[h])
            ns_ref[b, h] = _row_to_col(dec[:, sl]) * s_old + _mm_tn(kh[:, sl], vb16[:, sl])
        ohg_ref[b] = jnp.concatenate(o_parts, axis=1)

        kn = _rotary(zb[:, C_SK:C_SV], cos_t, sin_a, sin_b)
        vn = zb[:, C_SV:C_GA]
        kc = ck_ref[b]
        vc = cv_ref[b]
        kcr = pltpu.roll(kc, HEAD_DIM, 1)
        vcr = pltpu.roll(vc, HEAD_DIM, 1)
        knr = pltpu.roll(kn, HEAD_DIM, 1)
        vnr = pltpu.roll(vn, HEAD_DIM, 1)
        tiles = []
        for g in range(N_KV):
            ka, kb = _split_heads(kc, kcr, g)
            va, vb = _split_heads(vc, vcr, g)
            kcat = jnp.concatenate([ka, kb], axis=0).astype(BF16)
            vcat = jnp.concatenate([va, vb], axis=0).astype(BF16)
            kna, knb = _split_heads(kn, knr, g)
            vna, vnb = _split_heads(vn, vnr, g)
            for p in (2 * g, 2 * g + 1):
                qt = _rotary(zb[:, C_SQ + p * LANES:C_SQ + (p + 1) * LANES], cos_t, sin_a, sin_b)
                qt = qt * (HEAD_DIM ** -0.5)
                sc = _mm_nt(qt.astype(BF16), kcat)
                out = jnp.zeros((tn, LANES), F32)
                for half, (knh, vnh) in enumerate(((kna, vna), (knb, vnb))):
                    sink = sinks_ref[2 * p + half]
                    s_c = jnp.where(cvalid, sc[:, half * wb:(half + 1) * wb], NEG_INF)
                    s_n = []
                    for s in range(tn):
                        col = jnp.sum(qt * jnp.broadcast_to(knh[s:s + 1, :], (tn, LANES)), axis=1, keepdims=True)
                        s_n.append(jnp.where(trow >= s, col, NEG_INF))
                    mx = jnp.maximum(jnp.max(s_c, axis=1, keepdims=True), sink)
                    for s in range(tn):
                        mx = jnp.maximum(mx, s_n[s])
                    e_c = jnp.exp(s_c - mx)
                    den = jnp.sum(e_c, axis=1, keepdims=True) + jnp.exp(sink - mx)
                    e_n = []
                    for s in range(tn):
                        e_n.append(jnp.exp(s_n[s] - mx))
                        den = den + e_n[s]
                    inv = 1.0 / den
                    pc = (e_c * inv).astype(BF16)
                    o_h = _mm(pc, vcat[half * wb:(half + 1) * wb])
                    for s in range(tn):
                        o_h = o_h + (e_n[s] * inv) * jnp.broadcast_to(vnh[s:s + 1, :], (tn, LANES))
                    out = out + o_h
                tiles.append(out)
        ob_ref[b] = jnp.concatenate(tiles, axis=1)

        nk_ref[b, pl.ds(0, wb - tn), :] = ck_ref[b, pl.ds(tn, wb - tn), :]
        nk_ref[b, pl.ds(wb - tn, tn), :] = kn
        nv_ref[b, pl.ds(0, wb - tn), :] = cv_ref[b, pl.ds(tn, wb - tn), :]
        nv_ref[b, pl.ds(wb - tn, tn), :] = vn
        return carry

    lax.fori_loop(0, z_ref.shape[0], body, 0)


def _sample_core(z3, sinks, ck, cv, st, tabs, lb):
    nb, tn, _ = z3.shape
    wb = ck.shape[1]
    cos_t, sin_a, sin_b = tabs
    tab_spec = _const_spec((tn, LANES), 1)
    return pl.pallas_call(
        _sample_core_kernel,
        grid=(nb // BB,),
        in_specs=[
            _smem_spec(),
            pl.BlockSpec((BB, tn, D_IN), lambda i: (i, 0, 0)),
            pl.BlockSpec((BB, wb, KV_W), lambda i: (i, 0, 0)),
            pl.BlockSpec((BB, wb, KV_W), lambda i: (i, 0, 0)),
            pl.BlockSpec((BB, HG_HEADS, HG_DK, HG_DV), lambda i: (i, 0, 0, 0)),
            tab_spec, tab_spec, tab_spec,
            _const_spec((1, HG_W), 1),
        ],
        out_specs=[
            pl.BlockSpec((BB, tn, HG_W), lambda i: (i, 0, 0)),
            pl.BlockSpec((BB, tn, SWA_W), lambda i: (i, 0, 0)),
            pl.BlockSpec((BB, wb, KV_W), lambda i: (i, 0, 0)),
            pl.BlockSpec((BB, wb, KV_W), lambda i: (i, 0, 0)),
            pl.BlockSpec((BB, HG_HEADS, HG_DK, HG_DV), lambda i: (i, 0, 0, 0)),
        ],
        out_shape=[
            jax.ShapeDtypeStruct((nb, tn, HG_W), F32),
            jax.ShapeDtypeStruct((nb, tn, SWA_W), F32),
            jax.ShapeDtypeStruct((nb, wb, KV_W), F32),
            jax.ShapeDtypeStruct((nb, wb, KV_W), F32),
            jax.ShapeDtypeStruct((nb, HG_HEADS, HG_DK, HG_DV), F32),
        ],
        compiler_params=pltpu.CompilerParams(
            dimension_semantics=("arbitrary",), vmem_limit_bytes=VMEM_LIMIT),
        name="sample_core",
    )(sinks, z3, ck, cv, st, cos_t, sin_a, sin_b, lb)


def _sample_post_kernel(x_ref, z_ref, ohg_ref, ob_ref, hgw_ref, wua_ref, wub_ref, wo_ref, x1_ref):
    ya_in = _hgrn_norm_gate(ohg_ref[...], z_ref[:, C_HG:C_SQ], hgw_ref[...])
    x1_ref[...] = _merge_out(x_ref[...], ya_in, ob_ref[...], z_ref[:, C_GA:C_GB], z_ref[:, C_GB:],
                             wua_ref, wub_ref, wo_ref)


def _sample_post(x, z, ohg, ob, hgw, wua, wub, wo):
    rows = x.shape[0]
    return pl.pallas_call(
        _sample_post_kernel,
        grid=(1,),
        in_specs=[
            _const_spec((rows, D_MODEL), 1), _const_spec((rows, D_IN), 1),
            _const_spec((rows, HG_W), 1), _const_spec((rows, SWA_W), 1),
            _const_spec((1, HG_W), 1),
            _const_spec((HG_W, D_MODEL), 1),
            _const_spec((SWA_W, D_MODEL), 1),
            _const_spec((D_MODEL, D_MODEL), 1),
        ],
        out_specs=pl.BlockSpec((rows, D_MODEL), lambda i: (0, 0)),
        out_shape=jax.ShapeDtypeStruct((rows, D_MODEL), F32),
        compiler_params=pltpu.CompilerParams(
            dimension_semantics=("arbitrary",), vmem_limit_bytes=VMEM_LIMIT),
        name="sample_post",
    )(x, z, ohg, ob, hgw, wua, wub, wo)


def _rotary_tables(pos):
    inv = jnp.exp(-math.log(ROPE_THETA) * jnp.arange(ROT_HALF, dtype=F32) * (2.0 / ROT_DIM))
    ang = pos.astype(F32)[:, None] * inv[None, :]
    cos, sin = jnp.cos(ang), jnp.sin(ang)
    n = pos.shape[0]
    rest = HEAD_DIM - ROT_DIM
    ones = jnp.ones((n, rest), F32)
    zeros = jnp.zeros((n, rest), F32)
    z8 = jnp.zeros((n, ROT_HALF), F32)
    cos_h = jnp.concatenate([cos, cos, ones], axis=1)
    sa_h = jnp.concatenate([-sin, z8, zeros], axis=1)
    sb_h = jnp.concatenate([z8, sin, zeros], axis=1)
    rep = LANES // HEAD_DIM
    return tuple(jnp.tile(t, (1, rep)) for t in (cos_h, sa_h, sb_h))


def kernel(x_prompt, x_sample, cache_swa_k, cache_swa_v, state_hgrn, w_in, hgrn_lb_logits, hgrn_norm_w,
           sinks, w_up_a, w_up_b, w_o, norm1_w, norm2_w, w_ff1, w_ff2, normf_w):
    depth = w_in.shape[0]
    assert depth == 1
    bp, tp, _ = x_prompt.shape
    bd, tn, _ = x_sample.shape
    wb = cache_swa_k.shape[2]

    lb_all = jnp.cumsum(jax.nn.softmax(hgrn_lb_logits.astype(F32), axis=0), axis=0)
    lb = lb_all[0].reshape(1, HG_W)
    hgw = hgrn_norm_w[0].reshape(1, HG_W)
    n1 = norm1_w[0].reshape(1, D_MODEL)
    n2 = norm2_w[0].reshape(1, D_MODEL)
    nf = normf_w.reshape(1, D_MODEL)
    sk = sinks[0]
    win = w_in[0].astype(BF16)
    wua = w_up_a[0].astype(BF16)
    wub = w_up_b[0].astype(BF16)
    wo = w_o[0].astype(BF16)
    w1 = w_ff1[0].astype(BF16)
    w2 = w_ff2[0].astype(BF16)

    tabs_p = _rotary_tables(jnp.arange(tp))
    x1p, nkp, nvp, nsp = _mixer_prompt(x_prompt, sk, tabs_p, n1, win, lb, hgw, wua, wub, wo)
    y_prompt = _ffn(x1p.reshape(bp * tp, D_MODEL), n2, w1, w2, nf).reshape(bp, tp, D_MODEL)

    tabs_s = _rotary_tables(PAST_LEN + jnp.arange(tn))
    xs = x_sample.reshape(bd * tn, D_MODEL)
    zs = _sample_in(xs, n1, win)
    ohg, ob, nks, nvs, nss = _sample_core(
        zs.reshape(bd, tn, D_IN), sk, cache_swa_k[0].reshape(bd, wb, KV_W), cache_swa_v[0].reshape(bd, wb, KV_W),
        state_hgrn[0], tabs_s, lb)
    x1s = _sample_post(xs, zs, ohg.reshape(bd * tn, HG_W), ob.reshape(bd * tn, SWA_W), hgw, wua, wub, wo)
    y_sample = _ffn(x1s, n2, w1, w2, nf).reshape(bd, tn, D_MODEL)

    kv_shape_p = (1, bp, WINDOW, N_KV, HEAD_DIM)
    kv_shape_s = (1, bd, wb, N_KV, HEAD_DIM)
    return (y_prompt, y_sample,
            nkp.reshape(kv_shape_p), nvp.reshape(kv_shape_p), nsp[None],
            nks.reshape(kv_shape_s), nvs.reshape(kv_shape_s), nss[None])
```
